```python
import math
import jax, jax.numpy as jnp
from jax import lax
import numpy as np

D_MODEL = 1024
BATCH = 8
SEQ = 2048
DEPTH = 2

CHUNK = 64
Q_BLOCK = 128
N_MIXERS = 2
SB_HEADS = 16
SB_HEAD_DIM = D_MODEL // SB_HEADS
DIFF_HEADS = 8
DIFF_QK_DIM = D_MODEL // (2 * DIFF_HEADS)
DIFF_V_DIM = 2 * DIFF_QK_DIM
D_FF = 2816
ROPE_THETA = 10000.0
RMS_EPS = 1e-6
N_SB = (DEPTH + 1) // 2
N_DIFF = DEPTH // 2

kernel_name = "hybrid_stickbreak_diffattn_macaron"


def _rms_norm(x, g):
    xf = x.astype(jnp.float32)
    y = xf * lax.rsqrt(jnp.mean(xf * xf, axis=-1, keepdims=True) + RMS_EPS)
    return (y * g.astype(jnp.float32)).astype(x.dtype)


def _swiglu_ffn(h, w_in, w_out):
    gu = h @ w_in
    g, u = jnp.split(gu, 2, axis=-1)
    return (jax.nn.silu(g) * u) @ w_out


def _rope(x, pos):
    d = x.shape[-1]
    half = d // 2
    inv_freq = ROPE_THETA ** (-jnp.arange(half, dtype=jnp.float32) / half)
    ang = pos.astype(jnp.float32)[:, None] * inv_freq[None, :]
    cos, sin = jnp.cos(ang), jnp.sin(ang)
    xf = x.astype(jnp.float32)
    x1, x2 = xf[..., :half], xf[..., half:]
    out = jnp.concatenate([x1 * cos - x2 * sin, x2 * cos + x1 * sin], axis=-1)
    return out.astype(x.dtype)


def _stick_breaking_mixer(h, w_qkv, w_o):
    B, S, D = h.shape
    qkv = (h @ w_qkv).reshape(B, S, 3, SB_HEADS, SB_HEAD_DIM)
    qkv = jnp.transpose(qkv, (2, 0, 3, 1, 4))
    q, k, v = qkv[0], qkv[1], qkv[2]
    scale = 1.0 / math.sqrt(SB_HEAD_DIM)
    outs = []
    for blk in range(S // Q_BLOCK):
        q0 = blk * Q_BLOCK
        kend = q0 + Q_BLOCK
        qb = q[:, :, q0:kend]
        kb, vb = k[:, :, :kend], v[:, :, :kend]
        z = jnp.einsum('bhqd,bhkd->bhqk', qb, kb).astype(jnp.float32) * scale
        qpos = q0 + jnp.arange(Q_BLOCK)
        kpos = jnp.arange(kend)
        strict = kpos[None, :] < qpos[:, None]
        log_1m = jnp.where(strict, -jax.nn.softplus(z), 0.0)
        suffix = lax.cumsum(log_1m, axis=3, reverse=True) - log_1m
        log_a = jax.nn.log_sigmoid(z) + suffix
        a = jnp.where(strict, jnp.exp(log_a), 0.0)
        outs.append(jnp.einsum('bhqk,bhkd->bhqd', a, vb.astype(jnp.float32)))
    o = jnp.concatenate(outs, axis=2).astype(h.dtype)
    o = jnp.transpose(o, (0, 2, 1, 3)).reshape(B, S, D)
    return o @ w_o


def _diff_attention_mixer(h, w_qkv, w_o, lam_params, subln_g, layer_idx):
    B, S, D = h.shape
    lambda_init = 0.8 - 0.6 * math.exp(-0.3 * (layer_idx - 1))
    qk_w = DIFF_HEADS * 2 * DIFF_QK_DIM
    proj = h @ w_qkv
    q = proj[..., :qk_w].reshape(B, S, DIFF_HEADS, 2, DIFF_QK_DIM)
    k = proj[..., qk_w:2 * qk_w].reshape(B, S, DIFF_HEADS, 2, DIFF_QK_DIM)
    v = proj[..., 2 * qk_w:].reshape(B, S, DIFF_HEADS, DIFF_V_DIM)
    q = jnp.transpose(q, (3, 0, 2, 1, 4))
    k = jnp.transpose(k, (3, 0, 2, 1, 4))
    v = jnp.transpose(v, (0, 2, 1, 3))
    pos = jnp.arange(S)
    q1, q2 = _rope(q[0], pos), _rope(q[1], pos)
    k1, k2 = _rope(k[0], pos), _rope(k[1], pos)
    lp = lam_params.astype(jnp.float32)
    lam = jnp.exp(jnp.sum(lp[0] * lp[1])) - jnp.exp(jnp.sum(lp[2] * lp[3])) + lambda_init
    scale = 1.0 / math.sqrt(DIFF_QK_DIM)
    outs = []
    for blk in range(S // Q_BLOCK):
        q0 = blk * Q_BLOCK
        kend = q0 + Q_BLOCK
        qpos = q0 + jnp.arange(Q_BLOCK)
        kpos = jnp.arange(kend)
        mask = (kpos[None, :] // CHUNK) <= (qpos[:, None] // CHUNK)
        s1 = jnp.einsum('bhqd,bhkd->bhqk', q1[:, :, q0:kend], k1[:, :, :kend]).astype(jnp.float32) * scale
        s2 = jnp.einsum('bhqd,bhkd->bhqk', q2[:, :, q0:kend], k2[:, :, :kend]).astype(jnp.float32) * scale
        a1 = jax.nn.softmax(jnp.where(mask, s1, -jnp.inf), axis=-1)
        a2 = jax.nn.softmax(jnp.where(mask, s2, -jnp.inf), axis=-1)
        a = a1 - lam * a2
        outs.append(jnp.einsum('bhqk,bhkd->bhqd', a, v[:, :, :kend].astype(jnp.float32)))
    o = jnp.concatenate(outs, axis=2)
    o = o * lax.rsqrt(jnp.mean(o * o, axis=-1, keepdims=True) + RMS_EPS)
    o = o * subln_g.astype(jnp.float32) * (1.0 - lambda_init)
    o = jnp.transpose(o.astype(h.dtype), (0, 2, 1, 3)).reshape(B, S, DIFF_HEADS * DIFF_V_DIM)
    return o @ w_o


def setup_inputs(seed: int = 0) -> dict:
    key = jax.random.key(seed)
    ks = jax.random.split(key, 12)
    D, F = D_MODEL, D_FF
    x = jax.random.normal(ks[0], (BATCH, SEQ, D), jnp.float32)
    norm_gains = 1.0 + 0.02 * jax.random.normal(ks[1], (DEPTH, 3, D), jnp.float32)
    final_gain = 1.0 + 0.02 * jax.random.normal(ks[2], (D,), jnp.float32)
    ffn_w_in = jax.random.normal(ks[3], (DEPTH, 2, D, 2 * F), jnp.float32) * D ** -0.5
    ffn_w_out = jax.random.normal(ks[4], (DEPTH, 2, F, D), jnp.float32) * F ** -0.5
    sb_w_qkv = jax.random.normal(ks[5], (N_SB, D, 3 * D), jnp.float32) * D ** -0.5
    sb_w_o = jax.random.normal(ks[6], (N_SB, D, D), jnp.float32) * D ** -0.5
    diff_w_qkv = jax.random.normal(ks[7], (N_DIFF, D, 3 * D), jnp.float32) * D ** -0.5
    diff_w_o = jax.random.normal(ks[8], (N_DIFF, DIFF_HEADS * DIFF_V_DIM, D), jnp.float32) * D ** -0.5
    diff_lambda = 0.1 * jax.random.normal(ks[9], (N_DIFF, 4, DIFF_QK_DIM), jnp.float32)
    diff_subln = 1.0 + 0.02 * jax.random.normal(ks[10], (N_DIFF, DIFF_V_DIM), jnp.float32)
    return {"x": x, "norm_gains": norm_gains, "final_gain": final_gain,
            "ffn_w_in": ffn_w_in, "ffn_w_out": ffn_w_out,
            "sb_w_qkv": sb_w_qkv, "sb_w_o": sb_w_o,
            "diff_w_qkv": diff_w_qkv, "diff_w_o": diff_w_o,
            "diff_lambda": diff_lambda, "diff_subln": diff_subln}


def reference(x, norm_gains, final_gain, ffn_w_in, ffn_w_out, sb_w_qkv, sb_w_o,
              diff_w_qkv, diff_w_o, diff_lambda, diff_subln):
    h = x
    for i in range(DEPTH):
        h = h + 0.5 * _swiglu_ffn(_rms_norm(h, norm_gains[i, 0]), ffn_w_in[i, 0], ffn_w_out[i, 0])
        hn = _rms_norm(h, norm_gains[i, 1])
        j = i // N_MIXERS
        if i % N_MIXERS == 0:
            mix = _stick_breaking_mixer(hn, sb_w_qkv[j], sb_w_o[j])
        else:
            mix = _diff_attention_mixer(hn, diff_w_qkv[j], diff_w_o[j], diff_lambda[j],
                                        diff_subln[j], i + 1)
        h = h + mix
        h = h + 0.5 * _swiglu_ffn(_rms_norm(h, norm_gains[i, 2]), ffn_w_in[i, 1], ffn_w_out[i, 1])
    return _rms_norm(h, final_gain)
```

```python
import functools
import math

import jax
import jax.numpy as jnp
from jax import lax
from jax.experimental import pallas as pl
from jax.experimental.pallas import tpu as pltpu

CHUNK = 64
HEAD_DIM = 64
LANES = 128
ROPE_THETA = 10000.0
RMS_EPS = 1e-6
SB_DEAD_LOG_WEIGHT = 88.0
VMEM_LIMIT_BYTES = 56 * 1024 * 1024

_F32 = jnp.float32
_BF16 = jnp.bfloat16


def _dot(a, b):
    return jnp.dot(a, b, preferred_element_type=_F32)


def _dot_nt(a, b):
    return lax.dot_general(a, b, (((1,), (1,)), ((), ())), preferred_element_type=_F32)


def _rms_norm(x, g):
    return x * lax.rsqrt(jnp.mean(x * x, axis=-1, keepdims=True) + RMS_EPS) * g


def _ffn_kernel(*refs, d_ff, f_tile, has_pre, has_final):
    refs = list(refs)
    x_ref = refs.pop(0)
    if has_pre:
        o_ref = refs.pop(0)
        wo_ref = refs.pop(0)
    gain_ref = refs.pop(0)
    w_in_ref = refs.pop(0)
    w_out_ref = refs.pop(0)
    if has_final:
        fg_ref = refs.pop(0)
    out_ref = refs.pop(0)
    act_ref = refs.pop(0)

    x = x_ref[...]
    if has_pre:
        x = x + _dot(o_ref[...], wo_ref[...])
    h = _rms_norm(x, gain_ref[...]).astype(_BF16)
    for c in range(d_ff // f_tile):
        g = _dot(h, w_in_ref[:, c * f_tile:(c + 1) * f_tile])
        u = _dot(h, w_in_ref[:, d_ff + c * f_tile:d_ff + (c + 1) * f_tile])
        act = g * (1.0 / (1.0 + jnp.exp(-g))) * u
        act_ref[:, c * f_tile:(c + 1) * f_tile] = act.astype(_BF16)
    y = x + 0.5 * _dot(act_ref[...], w_out_ref[...])
    if has_final:
        y = _rms_norm(y, fg_ref[...])
    out_ref[...] = y


def _resident(shape):
    return pl.BlockSpec(shape, lambda i: (0,) * len(shape), pipeline_mode=pl.Buffered(1))


def _ffn(x, gain, w_in, w_out, *, pre=None, final_gain=None, row_tile=512, f_tile=256):
    t, d = x.shape
    d_ff = w_out.shape[0]
    row_tile = min(row_tile, t)
    assert t % row_tile == 0 and d_ff % f_tile == 0
    row_spec = pl.BlockSpec((row_tile, d), lambda i: (i, 0))
    args, specs = [x], [row_spec]
    if pre is not None:
        o, w_o = pre
        args += [o, w_o]
        specs += [pl.BlockSpec((row_tile, o.shape[1]), lambda i: (i, 0)), _resident(w_o.shape)]
    args += [gain.reshape(1, d), w_in, w_out]
    specs += [_resident((1, d)), _resident(w_in.shape), _resident(w_out.shape)]
    if final_gain is not None:
        args.append(final_gain.reshape(1, d))
        specs.append(_resident((1, d)))
    kern = functools.partial(_ffn_kernel, d_ff=d_ff, f_tile=f_tile,
                             has_pre=pre is not None, has_final=final_gain is not None)
    return pl.pallas_call(
        kern,
        grid=(t // row_tile,),
        in_specs=specs,
        out_specs=row_spec,
        out_shape=jax.ShapeDtypeStruct((t, d), _F32),
        scratch_shapes=[pltpu.VMEM((row_tile, d_ff), _BF16)],
        compiler_params=pltpu.CompilerParams(
            dimension_semantics=("arbitrary",), vmem_limit_bytes=VMEM_LIMIT_BYTES),
        name="ffn",
    )(*args)


def _rotate_half_pairs(x):
    lane = lax.broadcasted_iota(jnp.int32, x.shape, 1)
    ahead = pltpu.roll(x, LANES - HEAD_DIM // 2, 1)
    behind = pltpu.roll(x, HEAD_DIM // 2, 1)
    return jnp.where(lane % HEAD_DIM < HEAD_DIM // 2, ahead, behind)


def _qkv_kernel(*refs, d, rope):
    if rope:
        x_ref, gain_ref, w_ref, cos_ref, sin_ref, q_ref, k_ref, v_ref = refs
    else:
        x_ref, gain_ref, w_ref, q_ref, k_ref, v_ref = refs
    h = _rms_norm(x_ref[...], gain_ref[...]).astype(_BF16)
    scale = 1.0 / math.sqrt(HEAD_DIM)
    for part, dst in enumerate((q_ref, k_ref, v_ref)):
        y = _dot(h, w_ref[:, part * d:(part + 1) * d])
        if rope and part < 2:
            cos, sin = cos_ref[...], sin_ref[...]
            cols = []
            for c in range(d // LANES):
                yc = y[:, c * LANES:(c + 1) * LANES]
                cols.append(yc * cos + _rotate_half_pairs(yc) * sin)
            y = jnp.concatenate(cols, axis=1)
        if part == 0:
            y = y * scale
        dst[...] = y.astype(_BF16)


def _rope_tables(seq):
    half = HEAD_DIM // 2
    inv_freq = ROPE_THETA ** (-jnp.arange(half, dtype=_F32) / half)
    ang = jnp.arange(seq, dtype=_F32)[:, None] * inv_freq[None, :]
    cos, sin = jnp.cos(ang), jnp.sin(ang)
    reps = LANES // HEAD_DIM
    cos_t = jnp.tile(jnp.concatenate([cos, cos], axis=1), (1, reps))
    sin_t = jnp.tile(jnp.concatenate([-sin, sin], axis=1), (1, reps))
    return cos_t, sin_t


def _qkv(x, gain, w_qkv, *, seq, rope, row_tile=512):
    t, d = x.shape
    row_tile = min(row_tile, seq)
    assert seq % row_tile == 0
    row_spec = pl.BlockSpec((row_tile, d), lambda i: (i, 0))
    args = [x, gain.reshape(1, d), w_qkv]
    specs = [row_spec, _resident((1, d)), _resident(w_qkv.shape)]
    if rope:
        blocks_per_seq = seq // row_tile
        tab_spec = pl.BlockSpec((row_tile, LANES), lambda i: (i % blocks_per_seq, 0))
        args += list(_rope_tables(seq))
        specs += [tab_spec, tab_spec]
    out = jax.ShapeDtypeStruct((t, d), _BF16)
    return pl.pallas_call(
        functools.partial(_qkv_kernel, d=d, rope=rope),
        grid=(t // row_tile,),
        in_specs=specs,
        out_specs=[row_spec] * 3,
        out_shape=[out] * 3,
        compiler_params=pltpu.CompilerParams(
            dimension_semantics=("arbitrary",), vmem_limit_bytes=VMEM_LIMIT_BYTES),
        name="qkv_rope" if rope else "qkv",
    )(*args)


def _attention_call(kern, q, k, v, extra, extra_specs, *, batch, seq, blk, scratch, name):
    t, d = q.shape
    nblk = seq // blk
    q_spec = pl.BlockSpec((blk, LANES), lambda b, p, i: (b * nblk + i, p))
    kv_spec = pl.BlockSpec((seq, LANES), lambda b, p, i: (b, p))
    return pl.pallas_call(
        kern,
        grid=(batch, d // LANES, nblk),
        in_specs=[q_spec, kv_spec, kv_spec] + extra_specs,
        out_specs=q_spec,
        out_shape=jax.ShapeDtypeStruct((t, d), _BF16),
        scratch_shapes=scratch,
        compiler_params=pltpu.CompilerParams(
            dimension_semantics=("arbitrary",) * 3, vmem_limit_bytes=VMEM_LIMIT_BYTES),
        name=name,
    )(q, k, v, *extra)


def _sb_kernel(q_ref, k_ref, v_ref, o_ref, acc_ref, carry_ref, *, blk):
    i = pl.program_id(2)
    row = lax.broadcasted_iota(jnp.int32, (blk, blk), 0)
    col = lax.broadcasted_iota(jnp.int32, (blk, blk), 1)
    strictly_earlier = col < row
    later = jnp.where(row > col, 1.0, 0.0).astype(_BF16)
    later2 = jnp.concatenate([later, later], axis=0)
    lane = lax.broadcasted_iota(jnp.int32, (blk, LANES), 1)
    q_all = q_ref[...]

    def visit(qh, j, diagonal):
        start = pl.multiple_of(j * blk, blk)
        kb = k_ref[pl.ds(start, blk), :]
        vb = v_ref[pl.ds(start, blk), :]
        z = _dot_nt(qh, kb)
        sp = jnp.maximum(z, 0.0) + jnp.log(1.0 + jnp.exp(-jnp.abs(z)))
        log_beta = z - sp
        if diagonal:
            sp = jnp.where(strictly_earlier, sp, 0.0)
        hi = sp.astype(_BF16)
        lo = (sp - hi.astype(_F32)).astype(_BF16)
        suffix = _dot(jnp.concatenate([hi, lo], axis=1), later2)
        a = jnp.exp(log_beta - suffix)
        if diagonal:
            a = jnp.where(strictly_earlier, a, 0.0)
        pv = _dot(a.astype(_BF16), vb)
        if diagonal:
            acc_ref[...] = pv
            carry_ref[...] = jnp.sum(sp, axis=-1, keepdims=True)
        else:
            carry = carry_ref[...]
            acc_ref[...] += jnp.exp(-carry) * pv
            carry_ref[...] = carry + jnp.sum(sp, axis=-1, keepdims=True)

    outs = []
    for head in range(LANES // HEAD_DIM):
        in_head = (lane >= head * HEAD_DIM) & (lane < (head + 1) * HEAD_DIM)
        qh = jnp.where(in_head, q_all, jnp.zeros_like(q_all))
        visit(qh, i, True)

        def alive():
            return jnp.min(carry_ref[...]) < SB_DEAD_LOG_WEIGHT

        def cond(state):
            j, live = state
            return jnp.logical_and(j >= 0, live)

        def body(state):
            j, _ = state
            visit(qh, j, False)
            return j - 1, alive()

        lax.while_loop(cond, body, (i - 1, alive()))
        outs.append(acc_ref[...])
    out = outs[0]
    for head in range(1, len(outs)):
        out = jnp.where(lane >= head * HEAD_DIM, outs[head], out)
    o_ref[...] = out.astype(_BF16)


def _sb_attention(q, k, v, *, batch, seq, blk=256):
    blk = min(blk, seq)
    scratch = [pltpu.VMEM((blk, LANES), _F32), pltpu.VMEM((blk, 1), _F32)]
    return _attention_call(functools.partial(_sb_kernel, blk=blk), q, k, v, [], [],
                           batch=batch, seq=seq, blk=blk, scratch=scratch, name="sb_attn")


def _diff_kernel(q_ref, k_ref, v_ref, lam_ref, subln_ref, o_ref,
                 m_ref, l_ref, acc_ref, *, blk, lambda_init):
    i = pl.program_id(2)
    n_maps = LANES // HEAD_DIM
    row = lax.broadcasted_iota(jnp.int32, (blk, blk), 0)
    col = lax.broadcasted_iota(jnp.int32, (blk, blk), 1)
    visible = (col // CHUNK) <= (row // CHUNK)
    lane = lax.broadcasted_iota(jnp.int32, (blk, LANES), 1)
    q_all = q_ref[...]
    q_maps = [jnp.where((lane >= m * HEAD_DIM) & (lane < (m + 1) * HEAD_DIM),
                        q_all, jnp.zeros_like(q_all)) for m in range(n_maps)]

    def visit(j, diagonal):
        start = pl.multiple_of(j * blk, blk)
        kb = k_ref[pl.ds(start, blk), :]
        vb = v_ref[pl.ds(start, blk), :]
        for m in range(n_maps):
            s = _dot_nt(q_maps[m], kb)
            if diagonal:
                s = jnp.where(visible, s, -jnp.inf)
                m_new = jnp.max(s, axis=-1, keepdims=True)
                p = jnp.exp(s - m_new)
                l_ref[m] = jnp.sum(p, axis=-1, keepdims=True)
                acc_ref[m] = _dot(p.astype(_BF16), vb)
            else:
                m_old = m_ref[m]
                m_new = jnp.maximum(m_old, jnp.max(s, axis=-1, keepdims=True))
                p = jnp.exp(s - m_new)
                rescale = jnp.exp(m_old - m_new)
                l_ref[m] = rescale * l_ref[m] + jnp.sum(p, axis=-1, keepdims=True)
                acc_ref[m] = rescale * acc_ref[m] + _dot(p.astype(_BF16), vb)
            m_ref[m] = m_new

    visit(i, True)

    def body(step, carry):
        visit(i - 1 - step, False)
        return carry

    lax.fori_loop(0, i, body, 0)

    lp = lam_ref[...]
    lam = (jnp.exp(jnp.sum(lp[0:1] * lp[1:2], axis=-1, keepdims=True))
           - jnp.exp(jnp.sum(lp[2:3] * lp[3:4], axis=-1, keepdims=True)) + lambda_init)
    o = acc_ref[0] / l_ref[0] - lam * (acc_ref[1] / l_ref[1])
    o = o * lax.rsqrt(jnp.mean(o * o, axis=-1, keepdims=True) + RMS_EPS)
    o = o * subln_ref[...] * (1.0 - lambda_init)
    o_ref[...] = o.astype(_BF16)


def _diff_attention(q, k, v, lam_params, subln_g, *, batch, seq, layer_idx, blk=256):
    blk = min(blk, seq)
    assert blk % CHUNK == 0
    lambda_init = 0.8 - 0.6 * math.exp(-0.3 * (layer_idx - 1))
    n_maps = LANES // HEAD_DIM
    scratch = [pltpu.VMEM((n_maps, blk, 1), _F32), pltpu.VMEM((n_maps, blk, 1), _F32),
               pltpu.VMEM((n_maps, blk, LANES), _F32)]
    extra = [lam_params, subln_g.reshape(1, LANES)]
    extra_specs = [pl.BlockSpec(lam_params.shape, lambda b, p, i: (0, 0)),
                   pl.BlockSpec((1, LANES), lambda b, p, i: (0, 0))]
    kern = functools.partial(_diff_kernel, blk=blk, lambda_init=lambda_init)
    return _attention_call(kern, q, k, v, extra, extra_specs, batch=batch, seq=seq, blk=blk,
                           scratch=scratch, name="diff_attn")


def kernel(x, norm_gains, final_gain, ffn_w_in, ffn_w_out, sb_w_qkv, sb_w_o,
           diff_w_qkv, diff_w_o, diff_lambda, diff_subln):
    batch, seq, d = x.shape
    depth = norm_gains.shape[0]
    n_mixers = 2
    bf = lambda w: w.astype(_BF16)
    h = x.reshape(batch * seq, d)
    for i in range(depth):
        h = _ffn(h, norm_gains[i, 0], bf(ffn_w_in[i, 0]), bf(ffn_w_out[i, 0]))
        j = i // n_mixers
        if i % n_mixers == 0:
            q, k, v = _qkv(h, norm_gains[i, 1], bf(sb_w_qkv[j]), seq=seq, rope=False)
            o = _sb_attention(q, k, v, batch=batch, seq=seq)
            w_o = bf(sb_w_o[j])
        else:
            q, k, v = _qkv(h, norm_gains[i, 1], bf(diff_w_qkv[j]), seq=seq, rope=True)
            o = _diff_attention(q, k, v, diff_lambda[j], diff_subln[j],
                                batch=batch, seq=seq, layer_idx=i + 1)
            w_o = bf(diff_w_o[j])
        last = i == depth - 1
        h = _ffn(h, norm_gains[i, 2], bf(ffn_w_in[i, 1]), bf(ffn_w_out[i, 1]),
                 pre=(o, w_o), final_gain=final_gain if last else None)
    return h.reshape(batch, seq, d)
```
